```python
import math
import jax
import jax.numpy as jnp
from jax import lax
import numpy as np

D_MODEL = 1024
BATCH = 4
SEQ = 8192
DEPTH = 1

GRID_W = 64
CTX_LEN = 256
HEAD_DIM = 64
N_HEADS_DIFF = 8
N_HEADS_NA = 8
DIFF_QK_WIDTH = N_HEADS_DIFF * 2 * HEAD_DIM
DIFF_V_WIDTH = N_HEADS_DIFF * 2 * HEAD_DIM
NA_WIDTH = N_HEADS_NA * HEAD_DIM
IN_WIDTH = 2 * DIFF_QK_WIDTH + DIFF_V_WIDTH + 3 * NA_WIDTH + 2 * D_MODEL
NA_KH = 8
NA_KW = 16
Q_BLOCK = 128
ROPE_BASE = 10000.0
N_GROUPS = 4
EXPERTS_PER_GROUP = 8
N_EXPERTS = N_GROUPS * EXPERTS_PER_GROUP
TOP_K_IN_GROUP = 2
EXPERT_HIDDEN = 512
NORM_EPS = 1e-6

kernel_name = "hybrid_diffattn_natten_hmoe_block"


def rms_norm(x, w):
    xf = x.astype(jnp.float32)
    y = xf * lax.rsqrt(jnp.mean(xf * xf, axis=-1, keepdims=True) + NORM_EPS)
    return (y * w.astype(jnp.float32)).astype(x.dtype)


def adaln(cond, w_ada, b_ada):
    m = jax.nn.silu(cond) @ w_ada + b_ada
    return jnp.split(m, 6, axis=-1)


def modulate(h, shift, scale):
    return h * (1 + scale) + shift


def split_columns(p):
    widths = (DIFF_QK_WIDTH, DIFF_QK_WIDTH, DIFF_V_WIDTH, NA_WIDTH, NA_WIDTH, NA_WIDTH, D_MODEL, D_MODEL)
    points = [int(i) for i in np.cumsum(widths)[:-1]]
    return jnp.split(p, points, axis=-1)


def axial_rope_tables(n_tokens):
    t = jnp.arange(n_tokens, dtype=jnp.int32)
    pos = jnp.stack([t // GRID_W, t % GRID_W], axis=-1).astype(jnp.float32)
    n_freq = HEAD_DIM // 4
    inv_freq = ROPE_BASE ** (-jnp.arange(n_freq, dtype=jnp.float32) / n_freq)
    ang = pos[:, :, None] * inv_freq
    return jnp.cos(ang), jnp.sin(ang)


def apply_axial_rope(x, cos, sin):
    xr = x.reshape(*x.shape[:-1], 2, 2, HEAD_DIM // 4)
    x1, x2 = xr[..., 0, :], xr[..., 1, :]
    cs = cos[None, :, None].astype(x.dtype)
    sn = sin[None, :, None].astype(x.dtype)
    out = jnp.stack([x1 * cs - x2 * sn, x1 * sn + x2 * cs], axis=-2)
    return out.reshape(x.shape)


def diff_heads(q, k, v, q_norm, k_norm):
    b, l, _ = q.shape
    q = rms_norm(q.reshape(b, l, N_HEADS_DIFF, 2, HEAD_DIM), q_norm)
    k = rms_norm(k.reshape(b, l, N_HEADS_DIFF, 2, HEAD_DIM), k_norm)
    v = v.reshape(b, l, N_HEADS_DIFF, 2 * HEAD_DIM)
    return q[:, :, :, 0], q[:, :, :, 1], k[:, :, :, 0], k[:, :, :, 1], v


def diff_attend(q1, q2, k1, k2, v, lam):
    scale = HEAD_DIM ** -0.5
    p1 = jax.nn.softmax(jnp.einsum('bqhd,bkhd->bhqk', q1, k1).astype(jnp.float32) * scale, axis=-1)
    p2 = jax.nn.softmax(jnp.einsum('bqhd,bkhd->bhqk', q2, k2).astype(jnp.float32) * scale, axis=-1)
    p = (p1 - lam * p2).astype(v.dtype)
    return jnp.einsum('bhqk,bkhe->bqhe', p, v)


def diff_attention_blocks(q1, q2, k1, k2, v, lam):
    b, n, h, d = q1.shape
    nb = n // Q_BLOCK

    def blocks(q):
        return q.reshape(b, nb, Q_BLOCK, h, d).transpose(1, 0, 2, 3, 4)

    o = lax.map(lambda qs: diff_attend(qs[0], qs[1], k1, k2, v, lam), (blocks(q1), blocks(q2)))
    return o.transpose(1, 0, 2, 3, 4).reshape(b, n, h, 2 * d)


def softmax_attend(q, k, v):
    scale = HEAD_DIM ** -0.5
    p = jax.nn.softmax(jnp.einsum('bqhd,bkhd->bhqk', q, k).astype(jnp.float32) * scale, axis=-1)
    return jnp.einsum('bhqk,bkhd->bqhd', p.astype(v.dtype), v)


def neighbourhood_attention(q, k, v, k_ctx, v_ctx, rel_bias, rows):
    b, n, h, d = q.shape
    kh = min(NA_KH, rows)
    kw = NA_KW
    n_loc = kh * kw
    scale = d ** -0.5
    qg = q.reshape(b, rows, GRID_W, h, d)
    kg = k.reshape(b, rows, GRID_W, h, d)
    vg = v.reshape(b, rows, GRID_W, h, d)
    cols = np.arange(GRID_W)
    col_start = np.clip(cols - kw // 2, 0, GRID_W - kw)
    col_idx = col_start[:, None] + np.arange(kw)[None, :]
    col_bias_idx = col_idx - cols[:, None] + (NA_KW - 1)

    def row_block(r):
        r0 = jnp.clip(r - kh // 2, 0, rows - kh)
        q_row = lax.dynamic_index_in_dim(qg, r, axis=1, keepdims=False)
        k_rows = lax.dynamic_slice_in_dim(kg, r0, kh, axis=1)
        v_rows = lax.dynamic_slice_in_dim(vg, r0, kh, axis=1)
        k_win = k_rows[:, :, col_idx]
        v_win = v_rows[:, :, col_idx]
        row_bias_idx = r0 + jnp.arange(kh) - r + (NA_KH - 1)
        bias = rel_bias[:, row_bias_idx][:, :, col_bias_idx]
        s_loc = (jnp.einsum('bwhd,biwjhd->bhwij', q_row, k_win).astype(jnp.float32) * scale
                 + bias.transpose(0, 2, 1, 3)[None].astype(jnp.float32))
        s_ctx = jnp.einsum('bwhd,bchd->bhwc', q_row, k_ctx).astype(jnp.float32) * scale
        s = jnp.concatenate([s_loc.reshape(b, h, GRID_W, n_loc), s_ctx], axis=-1)
        p = jax.nn.softmax(s, axis=-1).astype(v.dtype)
        p_loc = p[..., :n_loc].reshape(b, h, GRID_W, kh, kw)
        p_ctx = p[..., n_loc:]
        return (jnp.einsum('bhwij,biwjhd->bwhd', p_loc, v_win)
                + jnp.einsum('bhwc,bchd->bwhd', p_ctx, v_ctx))

    o = lax.map(row_block, jnp.arange(rows, dtype=jnp.int32))
    return o.transpose(1, 0, 2, 3, 4).reshape(b, n, h * d)


def gated_merge(o_a, o_b, g_a, g_b, w_branch_a, w_branch_b, w_out):
    y = jax.nn.sigmoid(g_a) * (o_a @ w_branch_a) + jax.nn.sigmoid(g_b) * (o_b @ w_branch_b)
    return y @ w_out


def hierarchical_moe(h, w_rg, b_rg, w_re, b_re, w_gate, w_up, w_down):
    b, l, d = h.shape
    t = h.reshape(b * l, d)
    g_logits = (t @ w_rg + b_rg).astype(jnp.float32)
    g_prob = jax.nn.softmax(g_logits, axis=-1)
    g_idx = jnp.argmax(g_logits, axis=-1)
    g_w = jnp.max(g_prob, axis=-1, keepdims=True)
    e_logits = (jnp.einsum('td,gde->tge', t, w_re) + b_re).astype(jnp.float32)
    sel = jnp.einsum('tge,tg->te', e_logits, jax.nn.one_hot(g_idx, N_GROUPS, dtype=jnp.float32))
    top_vals, top_idx = lax.top_k(sel, TOP_K_IN_GROUP)
    weights = g_w * jax.nn.softmax(top_vals, axis=-1)
    expert_idx = g_idx[:, None] * EXPERTS_PER_GROUP + top_idx
    comb = jnp.einsum('tke,tk->te', jax.nn.one_hot(expert_idx, N_EXPERTS, dtype=jnp.float32), weights)
    comb = comb.astype(t.dtype)
    y = jnp.zeros_like(t)
    for e in range(N_EXPERTS):
        hid = jax.nn.silu(t @ w_gate[e]) * (t @ w_up[e])
        y = y + comb[:, e:e + 1] * (hid @ w_down[e])
    return y.reshape(b, l, d)


def setup_inputs(seed: int = 0) -> dict:
    key = jax.random.key(seed)
    ks = jax.random.split(key, 32)
    f32 = jnp.float32
    L, D = DEPTH, D_MODEL

    def nrm(k, shape, scale):
        return jax.random.normal(k, shape, f32) * scale

    def gain(k, shape):
        return 1.0 + 0.02 * jax.random.normal(k, shape, f32)

    return {
        "x": nrm(ks[0], (BATCH, SEQ, D), 1.0),
        "c": nrm(ks[1], (BATCH, D), 1.0),
        "ctx": nrm(ks[2], (BATCH, CTX_LEN, D), 1.0),
        "c_ctx": nrm(ks[3], (D,), 1.0),
        "w_ada": nrm(ks[4], (L, D, 6 * D), 0.5 * D ** -0.5),
        "b_ada": nrm(ks[5], (L, 6 * D), 0.02),
        "norm1_w": gain(ks[6], (L, D)),
        "norm2_w": gain(ks[7], (L, D)),
        "w_in": nrm(ks[8], (L, D, IN_WIDTH), D ** -0.5),
        "q_norm_a": gain(ks[9], (L, HEAD_DIM)),
        "k_norm_a": gain(ks[10], (L, HEAD_DIM)),
        "lambda_q1": nrm(ks[11], (L, HEAD_DIM), 0.1),
        "lambda_k1": nrm(ks[12], (L, HEAD_DIM), 0.1),
        "lambda_q2": nrm(ks[13], (L, HEAD_DIM), 0.1),
        "lambda_k2": nrm(ks[14], (L, HEAD_DIM), 0.1),
        "subln_a": gain(ks[15], (L, 2 * HEAD_DIM)),
        "q_norm_b": gain(ks[16], (L, HEAD_DIM)),
        "k_norm_b": gain(ks[17], (L, HEAD_DIM)),
        "na_rel_bias": nrm(ks[18], (L, N_HEADS_NA, 2 * NA_KH - 1, 2 * NA_KW - 1), 0.1),
        "w_branch_a": nrm(ks[19], (L, DIFF_V_WIDTH, D), DIFF_V_WIDTH ** -0.5),
        "w_branch_b": nrm(ks[20], (L, NA_WIDTH, D), NA_WIDTH ** -0.5),
        "w_out": nrm(ks[21], (L, D, D), D ** -0.5),
        "w_router_group": nrm(ks[22], (L, D, N_GROUPS), D ** -0.5),
        "b_router_group": nrm(ks[23], (L, N_GROUPS), 0.01),
        "w_router_expert": nrm(ks[24], (L, N_GROUPS, D, EXPERTS_PER_GROUP), D ** -0.5),
        "b_router_expert": nrm(ks[25], (L, N_GROUPS, EXPERTS_PER_GROUP), 0.01),
        "w_expert_gate": nrm(ks[26], (L, N_EXPERTS, D, EXPERT_HIDDEN), D ** -0.5),
        "w_expert_up": nrm(ks[27], (L, N_EXPERTS, D, EXPERT_HIDDEN), D ** -0.5),
        "w_expert_down": nrm(ks[28], (L, N_EXPERTS, EXPERT_HIDDEN, D), EXPERT_HIDDEN ** -0.5),
    }


def reference(x, c, ctx, c_ctx, w_ada, b_ada, norm1_w, norm2_w, w_in, q_norm_a, k_norm_a,
              lambda_q1, lambda_k1, lambda_q2, lambda_k2, subln_a, q_norm_b, k_norm_b, na_rel_bias,
              w_branch_a, w_branch_b, w_out, w_router_group, b_router_group, w_router_expert,
              b_router_expert, w_expert_gate, w_expert_up, w_expert_down):
    b, n, d = x.shape
    rows = n // GRID_W
    cos, sin = axial_rope_tables(n)
    for l in range(DEPTH):
        need_ctx = l < DEPTH - 1
        lam_init = 0.8 - 0.6 * math.exp(-0.3 * l)
        sh1, sc1, ga1, sh2, sc2, ga2 = adaln(c[:, None, :], w_ada[l], b_ada[l])
        csh1, csc1, cga1, csh2, csc2, cga2 = adaln(c_ctx[None, None, :], w_ada[l], b_ada[l])

        hx = modulate(rms_norm(x, norm1_w[l]), sh1, sc1)
        hc = modulate(rms_norm(ctx, norm1_w[l]), csh1, csc1)
        qa_x, ka_x, va_x, qb_x, kb_x, vb_x, gate_a_x, gate_b_x = split_columns(hx @ w_in[l])
        qa_c, ka_c, va_c, qb_c, kb_c, vb_c, gate_a_c, gate_b_c = split_columns(hc @ w_in[l])

        lam = (jnp.exp(jnp.sum(lambda_q1[l] * lambda_k1[l]).astype(jnp.float32))
               - jnp.exp(jnp.sum(lambda_q2[l] * lambda_k2[l]).astype(jnp.float32)) + lam_init)
        q1x, q2x, k1x, k2x, vax = diff_heads(qa_x, ka_x, va_x, q_norm_a[l], k_norm_a[l])
        q1c, q2c, k1c, k2c, vac = diff_heads(qa_c, ka_c, va_c, q_norm_a[l], k_norm_a[l])
        q1x, q2x = apply_axial_rope(q1x, cos, sin), apply_axial_rope(q2x, cos, sin)
        k1x, k2x = apply_axial_rope(k1x, cos, sin), apply_axial_rope(k2x, cos, sin)
        o_a = diff_attention_blocks(q1x, q2x,
                                    jnp.concatenate([k1c, k1x], axis=1),
                                    jnp.concatenate([k2c, k2x], axis=1),
                                    jnp.concatenate([vac, vax], axis=1), lam)
        o_a = (rms_norm(o_a, subln_a[l]) * (1 - lam_init)).reshape(b, n, DIFF_V_WIDTH)

        qbx = rms_norm(qb_x.reshape(b, n, N_HEADS_NA, HEAD_DIM), q_norm_b[l])
        kbx = rms_norm(kb_x.reshape(b, n, N_HEADS_NA, HEAD_DIM), k_norm_b[l])
        vbx = vb_x.reshape(b, n, N_HEADS_NA, HEAD_DIM)
        kbc = rms_norm(kb_c.reshape(b, CTX_LEN, N_HEADS_NA, HEAD_DIM), k_norm_b[l])
        vbc = vb_c.reshape(b, CTX_LEN, N_HEADS_NA, HEAD_DIM)
        o_b = neighbourhood_attention(qbx, kbx, vbx, kbc, vbc, na_rel_bias[l], rows)

        x_mid = x + ga1 * gated_merge(o_a, o_b, gate_a_x, gate_b_x, w_branch_a[l], w_branch_b[l], w_out[l])

        hx2 = modulate(rms_norm(x_mid, norm2_w[l]), sh2, sc2)
        x_new = x_mid + ga2 * hierarchical_moe(hx2, w_router_group[l], b_router_group[l], w_router_expert[l],
                                               b_router_expert[l], w_expert_gate[l], w_expert_up[l],
                                               w_expert_down[l])

        if need_ctx:
            o_a_c = diff_attend(q1c, q2c, k1c, k2c, vac, lam)
            o_a_c = (rms_norm(o_a_c, subln_a[l]) * (1 - lam_init)).reshape(b, CTX_LEN, DIFF_V_WIDTH)
            qbc = rms_norm(qb_c.reshape(b, CTX_LEN, N_HEADS_NA, HEAD_DIM), q_norm_b[l])
            o_b_c = softmax_attend(qbc, kbc, vbc).reshape(b, CTX_LEN, NA_WIDTH)
            ctx_mid = ctx + cga1 * gated_merge(o_a_c, o_b_c, gate_a_c, gate_b_c, w_branch_a[l], w_branch_b[l], w_out[l])
            hc2 = modulate(rms_norm(ctx_mid, norm2_w[l]), csh2, csc2)
            ctx = ctx_mid + cga2 * hierarchical_moe(hc2, w_router_group[l], b_router_group[l], w_router_expert[l],
                                                    b_router_expert[l], w_expert_gate[l], w_expert_up[l],
                                                    w_expert_down[l])
        x = x_new
    return x
```

```python
import functools
import math

import numpy as np
import jax
import jax.numpy as jnp
from jax import lax
from jax.experimental import pallas as pl
from jax.experimental.pallas import tpu as pltpu

F32 = jnp.float32
BF16 = jnp.bfloat16
HIGHEST = lax.Precision.HIGHEST

GRID_W = 64
HEAD_DIM = 64
N_HEADS = 8
NA_KH = 8
NA_KW = 16
ROPE_BASE = 10000.0
N_GROUPS = 4
EXPERTS_PER_GROUP = 8
N_EXPERTS = N_GROUPS * EXPERTS_PER_GROUP
EXPERT_HIDDEN = 512
NORM_EPS = 1e-6
NEG = -1e30

LANES = 128
MXU_DIM = 256
PROJ_TN = 512
ROUTER_LANE0 = N_GROUPS

COL_GATE_A = 0
COL_GATE_B = 8
COL_QA = 16
COL_KA = 24
COL_VA = 32
COL_QB = 40
COL_KB = 44
COL_VB = 48
IN_WIDTH = 52 * LANES


def _nt_dot(a, b):
    return lax.dot_general(a, b, (((1,), (1,)), ((), ())), preferred_element_type=F32)


def _adaln_kernel(c_ref, w_ref, b_ref, o_ref):
    cf = c_ref[...]
    s = cf * jax.nn.sigmoid(cf)
    o_ref[...] = jnp.dot(s, w_ref[...], preferred_element_type=F32, precision=HIGHEST) + b_ref[...]


def _adaln(cond, w, b):
    rows, d = cond.shape
    width = w.shape[1]
    tn = 1024
    return pl.pallas_call(
        _adaln_kernel,
        grid=(width // tn,),
        in_specs=[pl.BlockSpec((rows, d), lambda j: (0, 0)),
                  pl.BlockSpec((d, tn), lambda j: (0, j)),
                  pl.BlockSpec((1, tn), lambda j: (0, j))],
        out_specs=pl.BlockSpec((rows, tn), lambda j: (0, j)),
        out_shape=jax.ShapeDtypeStruct((rows, width), F32),
        name="adaln",
    )(cond, w, b)


def _proj_kernel(*refs, rope):
    if rope:
        (x_ref, sh_ref, sc_ref, nw_ref, w_ref, gain_ref, g_ref, cos_ref, sin_ref, o_ref, h_ref) = refs
    else:
        (x_ref, sh_ref, sc_ref, nw_ref, w_ref, gain_ref, g_ref, o_ref, h_ref) = refs
    j = pl.program_id(2)

    @pl.when(j == 0)
    def _():
        xf = x_ref[0]
        ms = jnp.mean(xf * xf, axis=-1, keepdims=True)
        y = xf * lax.rsqrt(ms + NORM_EPS) * nw_ref[...]
        h_ref[...] = (y * (1.0 + sc_ref[0]) + sh_ref[0]).astype(BF16)

    acc = jnp.dot(h_ref[...], w_ref[...], preferred_element_type=F32)

    def normed(c):
        a = acc[:, c * MXU_DIM:(c + 1) * MXU_DIM]
        ms = jnp.dot((a * a).astype(BF16), g_ref[...], preferred_element_type=F32)
        return a * lax.rsqrt(ms + NORM_EPS) * gain_ref[:, c * MXU_DIM:(c + 1) * MXU_DIM]

    j128 = j * (PROJ_TN // LANES)
    is_diff_qk = (j128 >= COL_QA) & (j128 < COL_VA)
    is_na_qk = (j128 >= COL_QB) & (j128 < COL_VB)

    if rope:
        @pl.when(is_diff_qk)
        def _():
            cs = cos_ref[...]
            sn = sin_ref[...]
            lane = lax.broadcasted_iota(jnp.int32, cs.shape, 1)
            low = (lane & 16) == 0
            for c in range(PROJ_TN // MXU_DIM):
                y = normed(c)
                for u in range(MXU_DIM // LANES):
                    yu = y[:, u * LANES:(u + 1) * LANES]
                    partner = jnp.where(low, pltpu.roll(yu, LANES - 16, 1), pltpu.roll(yu, 16, 1))
                    lo = c * MXU_DIM + u * LANES
                    o_ref[0, :, lo:lo + LANES] = (yu * cs + partner * sn).astype(BF16)

        norm_only = is_na_qk
    else:
        norm_only = is_diff_qk | is_na_qk

    @pl.when(norm_only)
    def _():
        for c in range(PROJ_TN // MXU_DIM):
            o_ref[0, :, c * MXU_DIM:(c + 1) * MXU_DIM] = normed(c).astype(BF16)

    @pl.when(jnp.logical_not(is_diff_qk | is_na_qk))
    def _():
        o_ref[0] = acc.astype(BF16)


def _project(x, shift, scale, norm_w, w_perm, gain_row, gmat, cos_t, sin_t, *, tm):
    b, n, d = x.shape
    per_batch = shift.shape[0] > 1
    rope = cos_t is not None
    mod_map = (lambda bi, i, j: (bi, 0, 0)) if per_batch else (lambda bi, i, j: (0, 0, 0))
    in_specs = [
        pl.BlockSpec((1, tm, d), lambda bi, i, j: (bi, i, 0)),
        pl.BlockSpec((1, 1, d), mod_map),
        pl.BlockSpec((1, 1, d), mod_map),
        pl.BlockSpec((1, d), lambda bi, i, j: (0, 0)),
        pl.BlockSpec((d, PROJ_TN), lambda bi, i, j: (0, j)),
        pl.BlockSpec((1, PROJ_TN), lambda bi, i, j: (0, j)),
        pl.BlockSpec((MXU_DIM, MXU_DIM), lambda bi, i, j: (0, 0)),
    ]
    args = [x, shift, scale, norm_w, w_perm, gain_row, gmat]
    if rope:
        in_specs += [pl.BlockSpec((tm, LANES), lambda bi, i, j: (i, 0)),
                     pl.BlockSpec((tm, LANES), lambda bi, i, j: (i, 0))]
        args += [cos_t, sin_t]
    return pl.pallas_call(
        functools.partial(_proj_kernel, rope=rope),
        grid=(b, n // tm, IN_WIDTH // PROJ_TN),
        in_specs=in_specs,
        out_specs=pl.BlockSpec((1, tm, PROJ_TN), lambda bi, i, j: (bi, i, j)),
        out_shape=jax.ShapeDtypeStruct((b, n, IN_WIDTH), BF16),
        scratch_shapes=[pltpu.VMEM((tm, d), BF16)],
        compiler_params=pltpu.CompilerParams(
            dimension_semantics=("arbitrary", "arbitrary", "arbitrary"),
            vmem_limit_bytes=40 * 1024 * 1024),
        name="proj_rope" if rope else "proj_ctx",
    )(*args)


def _diff_kernel(lam_ref, q_ref, kx_ref, vx_ref, kc_ref, vc_ref, sub_ref, o_ref,
                 qq_ref, m_ref, acc_ref, *, bq, bk, n_chunks, out_scale):
    q = q_ref[0]
    lane = lax.broadcasted_iota(jnp.int32, (bq, LANES), 1)
    qs = q.astype(F32) * (HEAD_DIM ** -0.5)
    qq_ref[0:bq, :] = jnp.where(lane < HEAD_DIM, qs, 0.0).astype(BF16)
    qq_ref[bq:2 * bq, :] = jnp.where(lane >= HEAD_DIM, qs, 0.0).astype(BF16)
    m_ref[...] = jnp.full(m_ref.shape, NEG, F32)
    acc_ref[...] = jnp.zeros(acc_ref.shape, F32)
    def ones_col(rows):
        return (lax.broadcasted_iota(jnp.int32, (rows, LANES), 1) == 0).astype(F32).astype(BF16)

    def chunk(kj, vj, ones):
        s = _nt_dot(qq_ref[...], kj)
        m_old = m_ref[...]
        m_new = jnp.maximum(m_old, jnp.max(s, axis=-1, keepdims=True))
        p = jnp.exp(s - m_new).astype(BF16)
        alpha = jnp.exp(m_old - m_new)
        vv = jnp.concatenate([vj, ones], axis=1)
        acc_ref[...] = alpha * acc_ref[...] + jnp.dot(p, vv, preferred_element_type=F32)
        m_ref[...] = m_new

    n_ctx = kc_ref.shape[1]
    chunk(kc_ref[0], vc_ref[0], ones_col(n_ctx))
    ones_bk = ones_col(bk)

    def body(i, carry):
        off = pl.multiple_of(i * bk, bk)
        chunk(kx_ref[0, pl.ds(off, bk), :], vx_ref[0, pl.ds(off, bk), :], ones_bk)
        return carry

    lax.fori_loop(0, n_chunks, body, 0)

    acc = acc_ref[...]
    o_all = acc[:, :LANES] / acc[:, LANES:LANES + 1]
    o = o_all[:bq] - lam_ref[0] * o_all[bq:]
    ms = jnp.mean(o * o, axis=-1, keepdims=True)
    o_ref[0] = (o * lax.rsqrt(ms + NORM_EPS) * (sub_ref[...] * out_scale)).astype(BF16)


def _diff_attention(lam, p_x, p_c, subln, *, out_scale, bq, bk):
    b, n, _ = p_x.shape
    n_ctx = p_c.shape[1]
    kern = functools.partial(_diff_kernel, bq=bq, bk=bk, n_chunks=n // bk, out_scale=out_scale)
    return pl.pallas_call(
        kern,
        grid=(b, N_HEADS, n // bq),
        in_specs=[
            pl.BlockSpec(memory_space=pltpu.SMEM),
            pl.BlockSpec((1, bq, LANES), lambda bi, h, i: (bi, i, COL_QA + h)),
            pl.BlockSpec((1, n, LANES), lambda bi, h, i: (bi, 0, COL_KA + h)),
            pl.BlockSpec((1, n, LANES), lambda bi, h, i: (bi, 0, COL_VA + h)),
            pl.BlockSpec((1, n_ctx, LANES), lambda bi, h, i: (bi, 0, COL_KA + h)),
            pl.BlockSpec((1, n_ctx, LANES), lambda bi, h, i: (bi, 0, COL_VA + h)),
            pl.BlockSpec((1, LANES), lambda bi, h, i: (0, 0)),
        ],
        out_specs=pl.BlockSpec((1, bq, LANES), lambda bi, h, i: (bi, i, h)),
        out_shape=jax.ShapeDtypeStruct((b, n, N_HEADS * LANES), BF16),
        scratch_shapes=[pltpu.VMEM((2 * bq, LANES), BF16),
                        pltpu.VMEM((2 * bq, 1), F32),
                        pltpu.VMEM((2 * bq, 2 * LANES), F32)],
        compiler_params=pltpu.CompilerParams(
            dimension_semantics=("arbitrary", "arbitrary", "arbitrary"),
            vmem_limit_bytes=48 * 1024 * 1024),
        name="diff_attn",
    )(lam, p_x, p_x, p_x, p_c, p_c, subln)


NA_QROWS = 8
NA_KROWS = 16
NA_BQ = NA_QROWS * GRID_W
NA_BK = NA_KROWS * GRID_W


def _na_kernel(q_ref, k_ref, v_ref, kc_ref, vc_ref, bias_ref, o_ref, *, n):
    g = pl.program_id(2)
    start = jnp.clip(g * NA_BQ - (NA_BK - NA_BQ) // 2, 0, n - NA_BK)
    start = pl.multiple_of(start, NA_BQ // 2)
    q = q_ref[0]
    lane = lax.broadcasted_iota(jnp.int32, (NA_BQ, LANES), 1)
    qs = q.astype(F32) * (HEAD_DIM ** -0.5)
    qq = jnp.concatenate([jnp.where(lane < HEAD_DIM, qs, 0.0),
                          jnp.where(lane >= HEAD_DIM, qs, 0.0)], axis=0).astype(BF16)
    kw = k_ref[0, pl.ds(start, NA_BK), :]
    vw = v_ref[0, pl.ds(start, NA_BK), :]
    s_loc = _nt_dot(qq, kw) + bias_ref[0, 0]
    s_ctx = _nt_dot(qq, kc_ref[0])
    m = jnp.maximum(jnp.max(s_loc, axis=-1, keepdims=True), jnp.max(s_ctx, axis=-1, keepdims=True))
    p_loc = jnp.exp(s_loc - m)
    p_ctx = jnp.exp(s_ctx - m)
    den = jnp.sum(p_loc, axis=-1, keepdims=True) + jnp.sum(p_ctx, axis=-1, keepdims=True)
    o = (jnp.dot(p_loc.astype(BF16), vw, preferred_element_type=F32)
         + jnp.dot(p_ctx.astype(BF16), vc_ref[0], preferred_element_type=F32)) / den
    o_ref[0] = jnp.where(lane < HEAD_DIM, o[:NA_BQ], o[NA_BQ:]).astype(BF16)


def _na_bias_table(rel_bias):
    qr = np.arange(NA_QROWS)[:, None, None, None]
    qc = np.arange(GRID_W)[None, :, None, None]
    kr = np.arange(NA_KROWS)[None, None, :, None]
    kc = np.arange(GRID_W)[None, None, None, :]
    cs = np.clip(qc - NA_KW // 2, 0, GRID_W - NA_KW)
    col_ok = (kc >= cs) & (kc < cs + NA_KW)
    ci = np.clip(kc - qc + NA_KW - 1, 0, 2 * NA_KW - 2)
    tabs = []
    for cls, qoff in enumerate((0, (NA_KROWS - NA_QROWS) // 2, NA_KROWS - NA_QROWS)):
        if cls == 0:
            lr0 = np.maximum(qr - NA_KH // 2, 0)
        elif cls == 1:
            lr0 = qr
        else:
            lr0 = np.minimum(qr + NA_KH // 2, NA_KROWS - NA_KH)
        row_ok = (kr >= lr0) & (kr < lr0 + NA_KH)
        ri = np.clip(kr - qoff - qr + NA_KH - 1, 0, 2 * NA_KH - 2)
        ok = np.broadcast_to(row_ok & col_ok, (NA_QROWS, GRID_W, NA_KROWS, GRID_W)).reshape(NA_BQ, NA_BK)
        rif = np.broadcast_to(ri, (NA_QROWS, GRID_W, NA_KROWS, GRID_W)).reshape(NA_BQ, NA_BK)
        cif = np.broadcast_to(ci, (NA_QROWS, GRID_W, NA_KROWS, GRID_W)).reshape(NA_BQ, NA_BK)
        flat = jnp.asarray(rif * (2 * NA_KW - 1) + cif, jnp.int32)
        vals = jnp.take(rel_bias.reshape(N_HEADS, -1), flat, axis=1)
        tabs.append(jnp.where(jnp.asarray(ok)[None], vals, NEG))
    t = jnp.stack(tabs)
    return t.reshape(3, N_HEADS // 2, 2 * NA_BQ, NA_BK)


def _na_attention(p_x, p_c, bias_tab):
    b, n, _ = p_x.shape
    n_ctx = p_c.shape[1]
    n_blk = n // NA_BQ

    def cls(g):
        return jnp.where(g == 0, 0, jnp.where(g == n_blk - 1, 2, 1))

    return pl.pallas_call(
        functools.partial(_na_kernel, n=n),
        grid=(N_HEADS // 2, b, n_blk),
        in_specs=[
            pl.BlockSpec((1, NA_BQ, LANES), lambda p, bi, g: (bi, g, COL_QB + p)),
            pl.BlockSpec((1, n, LANES), lambda p, bi, g: (bi, 0, COL_KB + p)),
            pl.BlockSpec((1, n, LANES), lambda p, bi, g: (bi, 0, COL_VB + p)),
            pl.BlockSpec((1, n_ctx, LANES), lambda p, bi, g: (bi, 0, COL_KB + p)),
            pl.BlockSpec((1, n_ctx, LANES), lambda p, bi, g: (bi, 0, COL_VB + p)),
            pl.BlockSpec((1, 1, 2 * NA_BQ, NA_BK), lambda p, bi, g: (cls(g), p, 0, 0)),
        ],
        out_specs=pl.BlockSpec((1, NA_BQ, LANES), lambda p, bi, g: (bi, g, p)),
        out_shape=jax.ShapeDtypeStruct((b, n, N_HEADS * HEAD_DIM), BF16),
        compiler_params=pltpu.CompilerParams(
            dimension_semantics=("arbitrary", "arbitrary", "arbitrary"),
            vmem_limit_bytes=56 * 1024 * 1024),
        name="na_attn",
    )(p_x, p_x, p_x, p_c, p_c, bias_tab)


def _merge_kernel(oa_ref, ob_ref, ga_ref, gb_ref, x_ref, g1_ref, sh2_ref, sc2_ref, nw2_ref,
                  wba_ref, wbb_ref, wout_ref, wr_ref, br_ref, xmid_ref, h2_ref, comb_ref):
    ya = jnp.dot(oa_ref[0], wba_ref[...], preferred_element_type=F32)
    yb = jnp.dot(ob_ref[0], wbb_ref[...], preferred_element_type=F32)
    y = jax.nn.sigmoid(ga_ref[0].astype(F32)) * ya + jax.nn.sigmoid(gb_ref[0].astype(F32)) * yb
    z = jnp.dot(y.astype(BF16), wout_ref[...], preferred_element_type=F32)
    xm = x_ref[0] + g1_ref[0] * z
    xmid_ref[0] = xm
    ms = jnp.mean(xm * xm, axis=-1, keepdims=True)
    h2 = (xm * lax.rsqrt(ms + NORM_EPS) * nw2_ref[...]) * (1.0 + sc2_ref[0]) + sh2_ref[0]
    h2_ref[0] = h2.astype(BF16)

    logits = jnp.dot(h2, wr_ref[...], preferred_element_type=F32, precision=HIGHEST) + br_ref[...]
    lane = lax.broadcasted_iota(jnp.int32, logits.shape, 1)
    lane_f = lane.astype(F32)
    gl = jnp.where(lane < N_GROUPS, logits, NEG)
    gmax = jnp.max(gl, axis=-1, keepdims=True)
    gidx = jnp.min(jnp.where(gl == gmax, lane_f, 1e9), axis=-1, keepdims=True)
    g_w = 1.0 / jnp.sum(jnp.exp(gl - gmax), axis=-1, keepdims=True)
    lo = ROUTER_LANE0 + EXPERTS_PER_GROUP * gidx
    el = jnp.where((lane_f >= lo) & (lane_f < lo + EXPERTS_PER_GROUP), logits, NEG)
    v1 = jnp.max(el, axis=-1, keepdims=True)
    i1 = jnp.min(jnp.where(el == v1, lane_f, 1e9), axis=-1, keepdims=True)
    el2 = jnp.where(lane_f == i1, NEG, el)
    v2 = jnp.max(el2, axis=-1, keepdims=True)
    i2 = jnp.min(jnp.where(el2 == v2, lane_f, 1e9), axis=-1, keepdims=True)
    t = jnp.exp(v2 - v1)
    w1 = g_w / (1.0 + t)
    w2 = g_w * t / (1.0 + t)
    comb_ref[...] = jnp.where(lane_f == i1, w1, 0.0) + jnp.where(lane_f == i2, w2, 0.0)


def _merge(o_a, o_b, p_x, x, gate1, shift2, scale2, norm2_w, wba, wbb, wout, w_router, b_router, *, tm):
    b, n, d = x.shape
    nt = n // tm
    row = lambda bi, i: (bi, i, 0)
    mod = lambda bi, i: (bi, 0, 0)
    const = lambda bi, i: (0, 0)
    return pl.pallas_call(
        _merge_kernel,
        grid=(b, nt),
        in_specs=[
            pl.BlockSpec((1, tm, o_a.shape[2]), row),
            pl.BlockSpec((1, tm, o_b.shape[2]), row),
            pl.BlockSpec((1, tm, d), lambda bi, i: (bi, i, COL_GATE_A // 8)),
            pl.BlockSpec((1, tm, d), lambda bi, i: (bi, i, COL_GATE_B // 8)),
            pl.BlockSpec((1, tm, d), row),
            pl.BlockSpec((1, 1, d), mod),
            pl.BlockSpec((1, 1, d), mod),
            pl.BlockSpec((1, 1, d), mod),
            pl.BlockSpec((1, d), const),
            pl.BlockSpec(wba.shape, const),
            pl.BlockSpec(wbb.shape, const),
            pl.BlockSpec(wout.shape, const),
            pl.BlockSpec(w_router.shape, const),
            pl.BlockSpec(b_router.shape, const),
        ],
        out_specs=[
            pl.BlockSpec((1, tm, d), row),
            pl.BlockSpec((1, tm, d), row),
            pl.BlockSpec((tm, LANES), lambda bi, i: (bi * nt + i, 0)),
        ],
        out_shape=[jax.ShapeDtypeStruct((b, n, d), F32),
                   jax.ShapeDtypeStruct((b, n, d), BF16),
                   jax.ShapeDtypeStruct((b * n, LANES), F32)],
        compiler_params=pltpu.CompilerParams(
            dimension_semantics=("arbitrary", "arbitrary"),
            vmem_limit_bytes=48 * 1024 * 1024),
        name="merge_router",
    )(o_a, o_b, p_x, p_x, x, gate1, shift2, scale2, norm2_w, wba, wbb, wout, w_router, b_router)


def _moe_kernel(h_ref, comb_ref, xmid_ref, g2_ref, wgu_ref, wd_ref, o_ref, acc_ref):
    e = pl.program_id(2)

    @pl.when(e == 0)
    def _():
        acc_ref[...] = jnp.zeros(acc_ref.shape, F32)

    hgu = jnp.dot(h_ref[0], wgu_ref[0], preferred_element_type=F32)
    hg = hgu[:, :EXPERT_HIDDEN]
    hu = hgu[:, EXPERT_HIDDEN:]
    hid = (hg * jax.nn.sigmoid(hg)) * hu
    y = jnp.dot(hid.astype(BF16), wd_ref[0], preferred_element_type=F32)
    comb = comb_ref[...]
    lane = lax.broadcasted_iota(jnp.int32, comb.shape, 1)
    cw = jnp.sum(jnp.where(lane == e + ROUTER_LANE0, comb, 0.0), axis=-1, keepdims=True)
    acc_ref[...] += cw * y

    @pl.when(e == N_EXPERTS - 1)
    def _():
        o_ref[0] = xmid_ref[0] + g2_ref[0] * acc_ref[...]


def _moe(h2, comb, x_mid, gate2, wgu, wd, *, tm):
    b, n, d = x_mid.shape
    nt = n // tm
    return pl.pallas_call(
        _moe_kernel,
        grid=(b, nt, N_EXPERTS),
        in_specs=[
            pl.BlockSpec((1, tm, d), lambda bi, i, e: (bi, i, 0)),
            pl.BlockSpec((tm, LANES), lambda bi, i, e: (bi * nt + i, 0)),
            pl.BlockSpec((1, tm, d), lambda bi, i, e: (bi, i, 0)),
            pl.BlockSpec((1, 1, d), lambda bi, i, e: (bi, 0, 0)),
            pl.BlockSpec((1, d, 2 * EXPERT_HIDDEN), lambda bi, i, e: (e, 0, 0)),
            pl.BlockSpec((1, EXPERT_HIDDEN, d), lambda bi, i, e: (e, 0, 0)),
        ],
        out_specs=pl.BlockSpec((1, tm, d), lambda bi, i, e: (bi, i, 0)),
        out_shape=jax.ShapeDtypeStruct((b, n, d), F32),
        scratch_shapes=[pltpu.VMEM((tm, d), F32)],
        compiler_params=pltpu.CompilerParams(
            dimension_semantics=("arbitrary", "arbitrary", "arbitrary"),
            vmem_limit_bytes=48 * 1024 * 1024),
        name="moe",
    )(h2, comb, x_mid, gate2, wgu, wd)


def _rope_tables(n):
    t = jnp.arange(n, dtype=jnp.int32)
    pos = jnp.stack([t // GRID_W, t % GRID_W], axis=-1).astype(F32)
    n_freq = HEAD_DIM // 4
    inv_freq = ROPE_BASE ** (-jnp.arange(n_freq, dtype=F32) / n_freq)
    ang = pos[:, :, None] * inv_freq
    cos, sin = jnp.cos(ang), jnp.sin(ang)
    cos64 = jnp.concatenate([cos[:, 0], cos[:, 0], cos[:, 1], cos[:, 1]], axis=-1)
    sin64 = jnp.concatenate([-sin[:, 0], sin[:, 0], -sin[:, 1], sin[:, 1]], axis=-1)
    return jnp.tile(cos64, (1, LANES // HEAD_DIM)), jnp.tile(sin64, (1, LANES // HEAD_DIM))


def kernel(x, c, ctx, c_ctx, w_ada, b_ada, norm1_w, norm2_w, w_in, q_norm_a, k_norm_a, lambda_q1, lambda_k1,
           lambda_q2, lambda_k2, subln_a, q_norm_b, k_norm_b, na_rel_bias, w_branch_a, w_branch_b, w_out,
           w_router_group, b_router_group, w_router_expert, b_router_expert, w_expert_gate, w_expert_up,
           w_expert_down):
    b, n, d = x.shape
    depth = w_ada.shape[0]
    cos_t, sin_t = _rope_tables(n)
    gmat = jnp.asarray(np.kron(np.eye(MXU_DIM // HEAD_DIM), np.full((HEAD_DIM, HEAD_DIM), 1.0 / HEAD_DIM)), BF16)
    for l in range(depth):
        lam_init = 0.8 - 0.6 * math.exp(-0.3 * l)
        cond = jnp.concatenate([c, c_ctx[None, :], jnp.zeros((8 - b - 1, d), F32)], axis=0)
        mod = _adaln(cond, w_ada[l], b_ada[l][None, :])
        sh1, sc1, ga1, sh2, sc2, ga2 = [m[:b, None, :] for m in jnp.split(mod, 6, axis=-1)]
        csh1, csc1 = [m[b:b + 1, None, :] for m in jnp.split(mod, 6, axis=-1)[:2]]

        w_perm = jnp.concatenate([w_in[l][:, 36 * LANES:], w_in[l][:, :36 * LANES]], axis=1).astype(BF16)
        ones = lambda k: jnp.ones((k,), F32)
        gain_row = jnp.concatenate([
            ones(16 * LANES), jnp.tile(q_norm_a[l], 16), jnp.tile(k_norm_a[l], 16), ones(8 * LANES),
            jnp.tile(q_norm_b[l], 8), jnp.tile(k_norm_b[l], 8), ones(4 * LANES)])[None, :]

        p_x = _project(x, sh1, sc1, norm1_w[l][None, :], w_perm, gain_row, gmat, cos_t, sin_t, tm=512)
        p_c = _project(ctx, csh1, csc1, norm1_w[l][None, :], w_perm, gain_row, gmat, None, None, tm=ctx.shape[1])

        lam = (jnp.exp(jnp.sum(lambda_q1[l] * lambda_k1[l])) - jnp.exp(jnp.sum(lambda_q2[l] * lambda_k2[l]))
               + lam_init).astype(F32).reshape(1)
        o_a = _diff_attention(lam, p_x, p_c, subln_a[l][None, :], out_scale=1.0 - lam_init, bq=512, bk=512)
        o_b = _na_attention(p_x, p_c, _na_bias_table(na_rel_bias[l]))

        w_router = jnp.concatenate(
            [w_router_group[l], w_router_expert[l].transpose(1, 0, 2).reshape(d, N_EXPERTS),
             jnp.zeros((d, LANES - N_GROUPS - N_EXPERTS), F32)], axis=1)
        b_router = jnp.concatenate(
            [b_router_group[l], b_router_expert[l].reshape(N_EXPERTS),
             jnp.zeros((LANES - N_GROUPS - N_EXPERTS,), F32)])[None, :]
        x_mid, h2, comb = _merge(o_a, o_b, p_x, x, ga1, sh2, sc2, norm2_w[l][None, :],
                                 w_branch_a[l].astype(BF16), w_branch_b[l].astype(BF16), w_out[l].astype(BF16),
                                 w_router, b_router, tm=512)

        wgu = jnp.concatenate([w_expert_gate[l], w_expert_up[l]], axis=-1).astype(BF16)
        x = _moe(h2, comb, x_mid, ga2, wgu, w_expert_down[l].astype(BF16), tm=1024)
        assert depth == 1
    return x
```
